```python
import math
import jax, jax.numpy as jnp
from jax import lax
import numpy as np

D_MODEL = 2048
BATCH = 4
SEQ = 2048
DEPTH = 2
DEC_BATCH = 128
DEC_SEQ = 1
PAST_LEN = 2048
PAGE_SIZE = 128

HEAD_DIM = 128
N_HEADS = D_MODEL // HEAD_DIM
SB_HEADS = N_HEADS // 2
MOBA_HEADS = N_HEADS - SB_HEADS
SB_WIDTH = SB_HEADS * HEAD_DIM
MOBA_WIDTH = MOBA_HEADS * HEAD_DIM
MIX_WIDTH = SB_WIDTH + MOBA_WIDTH
SB_Q_BLOCK = 128
MOBA_BLOCK = 256
MOBA_TOPK = 3
MOBA_Q_BLOCK = 16
REL_BUCKETS = 32
REL_MAX_DIST = 128
N_EXPERTS = 32
TOP_K = 4
D_FF = D_MODEL
SWIGLU_ALPHA = 1.702
SWIGLU_LIMIT = 7.0
MOE_BLOCK = 128
NORM_EPS = 1e-5

kernel_name = 'hymba_stickbreak_moba_moe_adaln_step'


def rms_norm(x, g):
    xf = x.astype(jnp.float32)
    y = xf * lax.rsqrt(jnp.mean(xf * xf, axis=-1, keepdims=True) + NORM_EPS)
    return (y * g.astype(jnp.float32)).astype(x.dtype)


def ada_params(c, w_ada, b_ada):
    m = jax.nn.silu(c) @ w_ada + b_ada
    return [t[:, None, :] for t in jnp.split(m, 6, axis=-1)]


def modulate(x, g, shift, scale):
    return rms_norm(x, g) * (1 + scale) + shift


def t5_bucket(dist):
    max_exact = REL_BUCKETS // 2
    n = jnp.maximum(dist, 0)
    nf = jnp.maximum(n, 1).astype(jnp.float32)
    large = max_exact + (jnp.log(nf / max_exact) / math.log(REL_MAX_DIST / max_exact)
                         * (REL_BUCKETS - max_exact)).astype(jnp.int32)
    large = jnp.minimum(large, REL_BUCKETS - 1)
    return jnp.where(n < max_exact, n, large)


def project_qkv(h, w_in):
    proj = h @ w_in
    cuts = [SB_WIDTH, 2 * SB_WIDTH, 3 * SB_WIDTH,
            3 * SB_WIDTH + MOBA_WIDTH, 3 * SB_WIDTH + 2 * MOBA_WIDTH]
    B, T = h.shape[:2]
    return [t.reshape(B, T, -1, HEAD_DIM) for t in jnp.split(proj, cuts, axis=-1)]


def stick_breaking_attend(q, k, v, q_pos, k_pos):
    z = jnp.einsum('bqhd,bkhd->bhqk', q, k, preferred_element_type=jnp.float32) * HEAD_DIM ** -0.5
    mask = k_pos[None, :] < q_pos[:, None]
    log_stay = jnp.where(mask, jax.nn.log_sigmoid(-z), 0.0)
    later = lax.cumsum(log_stay, axis=3, reverse=True) - log_stay
    a = jnp.where(mask, jnp.exp(jax.nn.log_sigmoid(z) + later), 0.0)
    return jnp.einsum('bhqk,bkhd->bqhd', a.astype(v.dtype), v)


def blockify(k):
    B, L, H, hd = k.shape
    nb = -(-L // MOBA_BLOCK)
    k = jnp.pad(k, ((0, 0), (0, nb * MOBA_BLOCK - L), (0, 0), (0, 0)))
    return k.reshape(B, nb, MOBA_BLOCK, H, hd).transpose(0, 3, 1, 2, 4)


def moba_attend(q, q_pos, kbt, vbt, kmean, rel_bias):
    B, H, NB = kbt.shape[:3]
    Tq = q.shape[1]
    n_sel = min(MOBA_TOPK, NB)
    own = q_pos // MOBA_BLOCK
    gate = jnp.einsum('bqhd,bhnd->bhqn', q.astype(jnp.float32), kmean)
    gate = jnp.where(jnp.arange(NB)[None, :] < own[:, None], gate, -jnp.inf)
    _, sel = lax.top_k(gate, n_sel)
    sel_ok = jnp.arange(n_sel)[None, :] < own[:, None]
    blk = jnp.concatenate([sel, jnp.broadcast_to(own[None, None, :, None], (B, H, Tq, 1))], axis=-1)
    blk_ok = jnp.concatenate([sel_ok, jnp.ones((Tq, 1), bool)], axis=-1)
    bi = jnp.arange(B)[:, None, None, None]
    hi = jnp.arange(H)[None, :, None, None]
    kg = kbt[bi, hi, blk]
    vg = vbt[bi, hi, blk]
    kpos = blk[..., None] * MOBA_BLOCK + jnp.arange(MOBA_BLOCK)
    dist = q_pos[None, None, :, None, None] - kpos
    valid = blk_ok[None, None, :, :, None] & (dist >= 0)
    bias = rel_bias.T.astype(jnp.float32)[hi[..., None], t5_bucket(dist)]
    logits = jnp.einsum('bqhd,bhqnpd->bhqnp', q, kg, preferred_element_type=jnp.float32) * HEAD_DIM ** -0.5 + bias
    logits = jnp.where(valid, logits, -jnp.inf)
    w = jax.nn.softmax(logits.reshape(B, H, Tq, -1), axis=-1).reshape(logits.shape)
    return jnp.einsum('bhqnp,bhqnpd->bqhd', w.astype(vg.dtype), vg)


def sweep_query_blocks(fn, q, q_pos, block):
    B, T = q.shape[:2]
    nb = T // block
    qb = q.reshape(B, nb, block, *q.shape[2:]).swapaxes(0, 1)
    pb = q_pos.reshape(nb, block)
    out = lax.map(lambda a: fn(a[0], a[1]), (qb, pb))
    return out.swapaxes(0, 1).reshape(q.shape)


def merge_groups(sb_o, mb_o, sb_out_g, moba_out_g, w_out):
    B, T = sb_o.shape[:2]
    sb = rms_norm(sb_o.reshape(B, T, SB_WIDTH), sb_out_g)
    mb = rms_norm(mb_o.reshape(B, T, MOBA_WIDTH), moba_out_g)
    return jnp.concatenate([sb, mb], axis=-1) @ w_out


def token_mix(h, q_pos, past, w_in, sb_out_g, moba_out_g, w_out, rel_bias):
    sq, sk, sv, mq, mk, mv = project_qkv(h, w_in)
    if past is None:
        sb_keys, sb_vals, mb_keys, mb_vals = sk, sv, mk, mv
    else:
        psk, psv, pmk, pmv = past
        sb_keys = jnp.concatenate([psk, sk], axis=1)
        sb_vals = jnp.concatenate([psv, sv], axis=1)
        mb_keys = jnp.concatenate([pmk, mk], axis=1)
        mb_vals = jnp.concatenate([pmv, mv], axis=1)
    k_pos = jnp.arange(sb_keys.shape[1], dtype=jnp.int32)
    kbt, vbt = blockify(mb_keys), blockify(mb_vals)
    kmean = jnp.mean(kbt.astype(jnp.float32), axis=3)
    sb_fn = lambda qb, pb: stick_breaking_attend(qb, sb_keys, sb_vals, pb, k_pos)
    mb_fn = lambda qb, pb: moba_attend(qb, pb, kbt, vbt, kmean, rel_bias)
    if past is None:
        sb_o = sweep_query_blocks(sb_fn, sq, q_pos, SB_Q_BLOCK)
        mb_o = sweep_query_blocks(mb_fn, mq, q_pos, MOBA_Q_BLOCK)
    else:
        sb_o = sb_fn(sq, q_pos)
        mb_o = mb_fn(mq, q_pos)
    return merge_groups(sb_o, mb_o, sb_out_g, moba_out_g, w_out), (sk, sv, mk, mv)


def moe_ffn(h, router_w, router_b, w_gu, b_gu, w_down, b_down):
    shp = h.shape
    x = h.reshape(-1, D_MODEL)
    T = x.shape[0]
    logits = (x @ router_w).astype(jnp.float32) + router_b.astype(jnp.float32)
    top_v, top_e = lax.top_k(logits, TOP_K)
    gates = jax.nn.softmax(top_v, axis=-1)
    flat_e = top_e.reshape(-1)
    flat_t = jnp.repeat(jnp.arange(T, dtype=jnp.int32), TOP_K)
    flat_g = gates.reshape(-1)
    order = jnp.argsort(flat_e)
    e_sorted = flat_e[order]
    counts = jnp.zeros((N_EXPERTS,), jnp.int32).at[flat_e].add(1)
    padded = (counts + MOE_BLOCK - 1) // MOE_BLOCK * MOE_BLOCK
    start = jnp.cumsum(counts) - counts
    pend = jnp.cumsum(padded)
    pstart = pend - padded
    dest = pstart[e_sorted] + jnp.arange(T * TOP_K, dtype=jnp.int32) - start[e_sorted]
    n_blocks = -(-(T * TOP_K + N_EXPERTS * (MOE_BLOCK - 1)) // MOE_BLOCK)
    n_rows = n_blocks * MOE_BLOCK
    tok = jnp.full((n_rows,), T, jnp.int32).at[dest].set(flat_t[order])
    gbuf = jnp.zeros((n_rows,), jnp.float32).at[dest].set(flat_g[order])
    block_e = jnp.minimum(jnp.searchsorted(pend, jnp.arange(n_blocks) * MOE_BLOCK, side='right'),
                          N_EXPERTS - 1)
    x_pad = jnp.concatenate([x, jnp.zeros((1, D_MODEL), x.dtype)], axis=0)
    xb = x_pad[tok].reshape(n_blocks, MOE_BLOCK, D_MODEL)

    def expert_block(args):
        xe, e = args
        gu = xe @ w_gu[e] + b_gu[e]
        g_lin, u_lin = gu[:, :D_FF], gu[:, D_FF:]
        g_lin = jnp.minimum(g_lin, SWIGLU_LIMIT)
        u_lin = jnp.clip(u_lin, -SWIGLU_LIMIT, SWIGLU_LIMIT)
        act = g_lin * jax.nn.sigmoid(SWIGLU_ALPHA * g_lin) * (u_lin + 1)
        return act @ w_down[e] + b_down[e]

    yb = lax.map(expert_block, (xb, block_e))
    y = jnp.zeros((T + 1, D_MODEL), jnp.float32).at[tok].add(
        yb.reshape(n_rows, D_MODEL).astype(jnp.float32) * gbuf[:, None])
    return y[:T].astype(h.dtype).reshape(shp)


def trunk_layer(x, c, q_pos, past, w_ada, b_ada, norm_attn_g, norm_ffn_g, w_in, sb_out_g,
                moba_out_g, w_out, rel_bias, router_w, router_b, w_gu, b_gu, w_down, b_down):
    sh1, sc1, g1, sh2, sc2, g2 = ada_params(c, w_ada, b_ada)
    mix, kv = token_mix(modulate(x, norm_attn_g, sh1, sc1), q_pos, past,
                        w_in, sb_out_g, moba_out_g, w_out, rel_bias)
    x = x + g1 * mix
    x = x + g2 * moe_ffn(modulate(x, norm_ffn_g, sh2, sc2), router_w, router_b, w_gu, b_gu, w_down, b_down)
    return x, kv


def gather_pages(cache, page_table, layer):
    pages = cache[page_table, layer]
    db, n_pages = page_table.shape
    return pages.reshape(db, n_pages * PAGE_SIZE, *pages.shape[3:])


def setup_inputs(seed: int = 0) -> dict:
    key = jax.random.key(seed)
    ks = iter(jax.random.split(key, 32))
    nrm = lambda shape, scale: jax.random.normal(next(ks), shape, jnp.float32) * scale
    n_pages = PAST_LEN // PAGE_SIZE
    n_used = DEC_BATCH * n_pages
    n_pool = n_used + n_used // 4
    page_table = jax.random.permutation(next(ks), n_pool)[:n_used].reshape(DEC_BATCH, n_pages).astype(jnp.int32)
    cache_shape = (n_pool, DEPTH, PAGE_SIZE, SB_HEADS, HEAD_DIM)
    moba_cache_shape = (n_pool, DEPTH, PAGE_SIZE, MOBA_HEADS, HEAD_DIM)
    return {
        'x_prompt': nrm((BATCH, SEQ, D_MODEL), 1.0),
        'x_sample': nrm((DEC_BATCH, DEC_SEQ, D_MODEL), 1.0),
        'cache_sb_k': nrm(cache_shape, 1.0),
        'cache_sb_v': nrm(cache_shape, 1.0),
        'cache_moba_k': nrm(moba_cache_shape, 1.0),
        'cache_moba_v': nrm(moba_cache_shape, 1.0),
        'page_table': page_table,
        'c_prompt': nrm((BATCH, D_MODEL), 1.0),
        'c_sample': nrm((DEC_BATCH, D_MODEL), 1.0),
        'w_ada': nrm((DEPTH, D_MODEL, 6 * D_MODEL), 0.5 * D_MODEL ** -0.5),
        'b_ada': nrm((DEPTH, 6 * D_MODEL), 0.02),
        'norm_attn_g': 1.0 + nrm((DEPTH, D_MODEL), 0.02),
        'norm_ffn_g': 1.0 + nrm((DEPTH, D_MODEL), 0.02),
        'w_in': nrm((DEPTH, D_MODEL, 3 * MIX_WIDTH), D_MODEL ** -0.5),
        'sb_out_g': 1.0 + nrm((DEPTH, SB_WIDTH), 0.02),
        'moba_out_g': 1.0 + nrm((DEPTH, MOBA_WIDTH), 0.02),
        'w_out': nrm((DEPTH, MIX_WIDTH, D_MODEL), MIX_WIDTH ** -0.5),
        'rel_bias': nrm((REL_BUCKETS, MOBA_HEADS), 0.2),
        'router_w': nrm((DEPTH, D_MODEL, N_EXPERTS), D_MODEL ** -0.5),
        'router_b': nrm((DEPTH, N_EXPERTS), 0.01),
        'w_gu': nrm((DEPTH, N_EXPERTS, D_MODEL, 2 * D_FF), D_MODEL ** -0.5),
        'b_gu': nrm((DEPTH, N_EXPERTS, 2 * D_FF), 0.01),
        'w_down': nrm((DEPTH, N_EXPERTS, D_FF, D_MODEL), D_FF ** -0.5),
        'b_down': nrm((DEPTH, N_EXPERTS, D_MODEL), 0.01),
        'final_g': 1.0 + nrm((D_MODEL,), 0.02),
    }


def reference(x_prompt, x_sample, cache_sb_k, cache_sb_v, cache_moba_k, cache_moba_v, page_table,
              c_prompt, c_sample, w_ada, b_ada, norm_attn_g, norm_ffn_g, w_in, sb_out_g, moba_out_g,
              w_out, rel_bias, router_w, router_b, w_gu, b_gu, w_down, b_down, final_g):
    past_len = page_table.shape[1] * PAGE_SIZE
    pos_p = jnp.arange(x_prompt.shape[1], dtype=jnp.int32)
    pos_s = past_len + jnp.arange(x_sample.shape[1], dtype=jnp.int32)
    hp, hs = x_prompt, x_sample
    kv_p, kv_s = [], []
    for l in range(DEPTH):
        lw = dict(w_ada=w_ada[l], b_ada=b_ada[l], norm_attn_g=norm_attn_g[l], norm_ffn_g=norm_ffn_g[l],
                  w_in=w_in[l], sb_out_g=sb_out_g[l], moba_out_g=moba_out_g[l], w_out=w_out[l],
                  rel_bias=rel_bias, router_w=router_w[l], router_b=router_b[l], w_gu=w_gu[l],
                  b_gu=b_gu[l], w_down=w_down[l], b_down=b_down[l])
        past = (gather_pages(cache_sb_k, page_table, l), gather_pages(cache_sb_v, page_table, l),
                gather_pages(cache_moba_k, page_table, l), gather_pages(cache_moba_v, page_table, l))
        hp, kvp = trunk_layer(hp, c_prompt, pos_p, None, **lw)
        hs, kvs = trunk_layer(hs, c_sample, pos_s, past, **lw)
        kv_p.append(kvp)
        kv_s.append(kvs)
    y_prompt = rms_norm(hp, final_g)
    y_sample = rms_norm(hs, final_g)
    new_sb_k_p = jnp.stack([kv[0] for kv in kv_p], axis=1)
    new_sb_v_p = jnp.stack([kv[1] for kv in kv_p], axis=1)
    new_mb_k_p = jnp.stack([kv[2] for kv in kv_p], axis=1)
    new_mb_v_p = jnp.stack([kv[3] for kv in kv_p], axis=1)
    new_sb_k_s = jnp.stack([kv[0] for kv in kv_s], axis=1)
    new_sb_v_s = jnp.stack([kv[1] for kv in kv_s], axis=1)
    new_mb_k_s = jnp.stack([kv[2] for kv in kv_s], axis=1)
    new_mb_v_s = jnp.stack([kv[3] for kv in kv_s], axis=1)
    return (y_prompt, y_sample, new_sb_k_p, new_sb_v_p, new_mb_k_p, new_mb_v_p,
            new_sb_k_s, new_sb_v_s, new_mb_k_s, new_mb_v_s)
```

```python
import functools
import math

import jax
import jax.numpy as jnp
import numpy as np
from jax import lax
from jax.experimental import pallas as pl
from jax.experimental.pallas import tpu as pltpu

F32 = jnp.float32
BF16 = jnp.bfloat16
I32 = jnp.int32

HEAD_DIM = 128
GROUP_HEADS = 8
GROUP_WIDTH = GROUP_HEADS * HEAD_DIM
MOBA_BLOCK = 256
MOBA_TOPK = 3
REL_BUCKETS = 32
REL_MAX_DIST = 128
N_EXPERTS = 32
TOP_K = 4
SWIGLU_ALPHA = 1.702
SWIGLU_LIMIT = 7.0
NORM_EPS = 1e-5
ATTN_SCALE = HEAD_DIM ** -0.5

V7X_VMEM_BYTES = 64 * 1024 * 1024
VMEM_CAP = V7X_VMEM_BYTES - 6 * 1024 * 1024

MOE_ROWS = 256
SB_TILE = 256


def _params(vmem_bytes, n_grid):
    return pltpu.CompilerParams(
        dimension_semantics=("arbitrary",) * n_grid,
        vmem_limit_bytes=int(min(VMEM_CAP, vmem_bytes)),
    )


def _dot(a, b):
    return jnp.dot(a, b, preferred_element_type=F32)


def _dot_nt(a, b):
    return lax.dot_general(a, b, (((1,), (1,)), ((), ())), preferred_element_type=F32)


def _split(x):
    hi = x.astype(BF16)
    lo = (x - hi.astype(F32)).astype(BF16)
    return hi, lo


def _dot_split(a, b):
    ah, al = _split(a)
    bh, bl = _split(b)
    return _dot(ah, bh) + _dot(ah, bl) + _dot(al, bh)


def _dot_nt_split(a, b):
    ah, al = _split(a)
    bh, bl = _split(b)
    return _dot_nt(ah, bh) + _dot_nt(ah, bl) + _dot_nt(al, bh)


def _softplus(z):
    return jnp.maximum(z, 0.0) + jnp.log(1.0 + jnp.exp(-jnp.abs(z)))


def _suffix_matrix(n):
    r = lax.broadcasted_iota(I32, (n, n), 0)
    c = lax.broadcasted_iota(I32, (n, n), 1)
    return (r > c).astype(BF16)


def _suffix_sum(x, u):
    hi, lo = _split(x)
    return _dot(hi, u) + _dot(lo, u)


def _rms(x):
    return x * lax.rsqrt(jnp.mean(x * x, axis=-1, keepdims=True) + NORM_EPS)


def _bias_from_dist(d, thr_ref, rb_at):
    acc = jnp.where(d >= thr_ref[1], rb_at(1), rb_at(0))
    for k in range(2, REL_BUCKETS):
        acc = jnp.where(d >= thr_ref[k], rb_at(k), acc)
    return acc


def _ada_kernel(c_ref, w_ref, b_ref, o_ref):
    c = c_ref[...]
    s = (c * jax.nn.sigmoid(c)).astype(BF16)
    o_ref[...] = _dot(s, w_ref[...].astype(BF16)) + b_ref[...]


def _ada_params(c_all, w_ada, b_ada):
    n_layers, d, n6 = w_ada.shape
    rows = c_all.shape[0]
    tn = 1024
    vmem = 2 * (d * tn * 4 + rows * tn * 4 + tn * 4) + 2 * rows * d * 4 + d * tn * 2 + rows * d * 2
    return pl.pallas_call(
        _ada_kernel,
        grid=(n_layers, n6 // tn),
        in_specs=[
            pl.BlockSpec((rows, d), lambda l, j: (0, 0)),
            pl.BlockSpec((None, d, tn), lambda l, j: (l, 0, j)),
            pl.BlockSpec((None, 1, tn), lambda l, j: (l, 0, j)),
        ],
        out_specs=pl.BlockSpec((None, rows, tn), lambda l, j: (l, 0, j)),
        out_shape=jax.ShapeDtypeStruct((n_layers, rows, n6), F32),
        compiler_params=_params(vmem + (8 << 20), 2),
        name="ada_params",
    )(c_all, w_ada, b_ada.reshape(n_layers, 1, n6))


def _modulate_kernel(x_ref, sh_ref, sc_ref, g_ref, o_ref):
    y = _rms(x_ref[...]) * g_ref[...]
    o_ref[...] = (y * (1.0 + sc_ref[...]) + sh_ref[...]).astype(o_ref.dtype)


def _modulate(x3, shift3, scale3, gain, layer, tm):
    g, r, d = x3.shape
    rm = shift3.shape[1]
    mod_block = (None, 1, d) if rm == 1 else (None, tm, d)
    mod_map = (lambda b, i: (b, 0, 0)) if rm == 1 else (lambda b, i: (b, i, 0))
    vmem = 2 * (tm * d * 4 + 2 * tm * d * 4 + tm * d * 2) + 4 * tm * d * 4
    return pl.pallas_call(
        _modulate_kernel,
        grid=(g, r // tm),
        in_specs=[
            pl.BlockSpec((None, tm, d), lambda b, i: (b, i, 0)),
            pl.BlockSpec(mod_block, mod_map),
            pl.BlockSpec(mod_block, mod_map),
            pl.BlockSpec((None, 1, d), lambda b, i: (layer, 0, 0)),
        ],
        out_specs=pl.BlockSpec((None, tm, d), lambda b, i: (b, i, 0)),
        out_shape=jax.ShapeDtypeStruct((g, r, d), BF16),
        compiler_params=_params(vmem + (4 << 20), 2),
        name="modulate",
    )(x3, shift3, scale3, gain.reshape(gain.shape[0], 1, d))


def _qkv_kernel(a_ref, w_ref, o_ref, wbf_ref):
    @pl.when(pl.program_id(1) == 0)
    def _():
        wbf_ref[...] = w_ref[...].astype(BF16)

    o_ref[...] = _dot(a_ref[...], wbf_ref[...])


def _qkv_proj(h, w_in, layer, tm):
    m, d = h.shape
    tn = GROUP_WIDTH
    n_slabs = w_in.shape[2] // tn
    vmem = 2 * (d * tn * 4 + tm * d * 2 + tm * tn * 4) + d * tn * 2
    return pl.pallas_call(
        _qkv_kernel,
        grid=(n_slabs, m // tm),
        in_specs=[
            pl.BlockSpec((tm, d), lambda j, i: (i, 0)),
            pl.BlockSpec((None, d, tn), lambda j, i: (layer, 0, j)),
        ],
        out_specs=pl.BlockSpec((None, tm, tn), lambda j, i: (j, i, 0)),
        out_shape=jax.ShapeDtypeStruct((n_slabs, m, tn), F32),
        scratch_shapes=[pltpu.VMEM((d, tn), BF16)],
        compiler_params=_params(vmem + (8 << 20), 2),
        name="qkv_proj",
    )(h, w_in)


def _sb_kernel(q_ref, k_ref, v_ref, o_ref, *, tile):
    qi = pl.program_id(2)
    q = q_ref[...].astype(BF16)
    u = _suffix_matrix(tile)
    r = lax.broadcasted_iota(I32, (tile, tile), 0)
    c = lax.broadcasted_iota(I32, (tile, tile), 1)
    causal = c < r

    def scores(j):
        start = pl.multiple_of(j * tile, tile)
        ks = k_ref[pl.ds(start, tile), :].astype(BF16)
        vs = v_ref[pl.ds(start, tile), :].astype(BF16)
        z = _dot_nt(q, ks) * ATTN_SCALE
        return z, _softplus(z), vs

    z, sp, vs = scores(qi)
    ls = jnp.where(causal, -sp, 0.0)
    loc = _suffix_sum(ls, u)
    a = jnp.where(causal, jnp.exp(z - sp + loc), 0.0)
    acc = _dot(a.astype(BF16), vs)
    carry = loc[:, :1] + ls[:, :1]

    def body(s, state):
        carry, acc = state
        z, sp, vs = scores(qi - 1 - s)
        ls = -sp
        loc = _suffix_sum(ls, u)
        a = jnp.exp(z - sp + (loc + carry))
        acc = acc + _dot(a.astype(BF16), vs)
        carry = carry + loc[:, :1] + ls[:, :1]
        return carry, acc

    _, acc = lax.fori_loop(0, qi, body, (carry, acc))
    o_ref[...] = acc


def _sb_attention(qkv4, tile):
    _, b, t, w = qkv4.shape
    vmem = 2 * (2 * t * HEAD_DIM * 4 + 2 * tile * HEAD_DIM * 4) + 16 * tile * tile * 4
    return pl.pallas_call(
        functools.partial(_sb_kernel, tile=tile),
        grid=(b, GROUP_HEADS, t // tile),
        in_specs=[
            pl.BlockSpec((None, None, tile, HEAD_DIM), lambda bi, h, qi: (0, bi, qi, h)),
            pl.BlockSpec((None, None, t, HEAD_DIM), lambda bi, h, qi: (1, bi, 0, h)),
            pl.BlockSpec((None, None, t, HEAD_DIM), lambda bi, h, qi: (2, bi, 0, h)),
        ],
        out_specs=pl.BlockSpec((None, tile, HEAD_DIM), lambda bi, h, qi: (bi, qi, h)),
        out_shape=jax.ShapeDtypeStruct((b, t, w), F32),
        compiler_params=_params(vmem + (8 << 20), 3),
        name="sb_attention",
    )(qkv4, qkv4, qkv4)


def _top_blocks(g, n_valid, n_blocks):
    cols = lax.broadcasted_iota(I32, g.shape, 1)
    valid = cols < n_valid
    gm = jnp.where(valid, g, -jnp.inf)
    rank = jnp.zeros(g.shape, I32)
    for m in range(n_blocks):
        gc = gm[:, m:m + 1]
        beats = (gc > gm) | ((gc == gm) & (m < cols))
        rank = rank + beats.astype(I32)
    return valid & (rank < MOBA_TOPK)


def _moba_kernel(thr_ref, rb_ref, q_ref, k_ref, v_ref, o_ref, bias_ref, kmean_ref, *, blk, n_blocks):
    h = pl.program_id(0)
    bi = pl.program_id(1)
    qi = pl.program_id(2)

    @pl.when((bi == 0) & (qi == 0))
    def _():
        r = lax.broadcasted_iota(I32, (blk, blk), 0)
        c = lax.broadcasted_iota(I32, (blk, blk), 1)
        for t in range(2):
            bias_ref[t] = _bias_from_dist(t * blk + r - c, thr_ref, lambda k: rb_ref[k, h])

    @pl.when(qi == 0)
    def _():
        kmean_ref[...] = jnp.zeros_like(kmean_ref)
        for n in range(n_blocks):
            kmean_ref[n:n + 1, :] = jnp.mean(k_ref[n * blk:(n + 1) * blk, :], axis=0, keepdims=True)

    qf = q_ref[...]
    q = qf.astype(BF16)
    sel = _top_blocks(_dot_nt_split(qf, kmean_ref[...])[:, :n_blocks], qi, n_blocks)
    sel_f = sel.astype(F32)
    cols = lax.broadcasted_iota(I32, sel.shape, 1)
    far_bias = rb_ref[REL_BUCKETS - 1, h]

    def block_logits(n):
        start = pl.multiple_of(n * blk, blk)
        ks = k_ref[pl.ds(start, blk), :].astype(BF16)
        vs = v_ref[pl.ds(start, blk), :].astype(BF16)
        return _dot_nt(q, ks) * ATTN_SCALE, vs

    r = lax.broadcasted_iota(I32, (blk, blk), 0)
    c = lax.broadcasted_iota(I32, (blk, blk), 1)
    z, vs = block_logits(qi)
    z = jnp.where(c <= r, z + bias_ref[0], -jnp.inf)
    m = jnp.max(z, axis=1, keepdims=True)
    p = jnp.exp(z - m)
    l = jnp.sum(p, axis=1, keepdims=True)
    acc = _dot(p.astype(BF16), vs)

    def body(n, state):
        m, l, acc = state
        z, vs = block_logits(n)
        bias = jnp.where(n == qi - 1, bias_ref[1], far_bias)
        keep = jnp.sum(jnp.where(cols == n, sel_f, 0.0), axis=1, keepdims=True) > 0.5
        z = jnp.where(keep, z + bias, -jnp.inf)
        m_new = jnp.maximum(m, jnp.max(z, axis=1, keepdims=True))
        alpha = jnp.exp(m - m_new)
        p = jnp.exp(z - m_new)
        l = alpha * l + jnp.sum(p, axis=1, keepdims=True)
        acc = alpha * acc + _dot(p.astype(BF16), vs)
        return m_new, l, acc

    m, l, acc = lax.fori_loop(0, qi, body, (m, l, acc))
    o_ref[...] = acc / l


def _moba_attention(qkv4, thr, rel_bias):
    _, b, t, w = qkv4.shape
    blk = MOBA_BLOCK
    n_blocks = t // blk
    vmem = 2 * (2 * t * HEAD_DIM * 4 + 2 * blk * HEAD_DIM * 4) + 2 * blk * blk * 4 + 16 * blk * blk * 4
    grid_spec = pltpu.PrefetchScalarGridSpec(
        num_scalar_prefetch=1,
        grid=(GROUP_HEADS, b, n_blocks),
        in_specs=[
            pl.BlockSpec(memory_space=pltpu.SMEM),
            pl.BlockSpec((None, None, blk, HEAD_DIM), lambda h, bi, qi, thr: (3, bi, qi, h)),
            pl.BlockSpec((None, None, t, HEAD_DIM), lambda h, bi, qi, thr: (4, bi, 0, h)),
            pl.BlockSpec((None, None, t, HEAD_DIM), lambda h, bi, qi, thr: (5, bi, 0, h)),
        ],
        out_specs=pl.BlockSpec((None, blk, HEAD_DIM), lambda h, bi, qi, thr: (bi, qi, h)),
        scratch_shapes=[pltpu.VMEM((2, blk, blk), F32), pltpu.VMEM((HEAD_DIM, HEAD_DIM), F32)],
    )
    return pl.pallas_call(
        functools.partial(_moba_kernel, blk=blk, n_blocks=n_blocks),
        grid_spec=grid_spec,
        out_shape=jax.ShapeDtypeStruct((b, t, w), F32),
        compiler_params=_params(vmem + (8 << 20), 3),
        name="moba_attention",
    )(thr, rel_bias, qkv4, qkv4, qkv4)


def _block_diag(row, dtype):
    shape = (GROUP_HEADS, GROUP_WIDTH)
    head_of_lane = lax.broadcasted_iota(I32, shape, 1) // HEAD_DIM
    head_of_row = lax.broadcasted_iota(I32, shape, 0)
    return jnp.where(head_of_lane == head_of_row, jnp.broadcast_to(row, shape), 0.0).astype(dtype)


def _diag_rows(x):
    head_of_lane = lax.broadcasted_iota(I32, x.shape, 1) // HEAD_DIM
    head_of_row = lax.broadcasted_iota(I32, x.shape, 0)
    return jnp.sum(jnp.where(head_of_lane == head_of_row, x, 0.0), axis=0, keepdims=True)


def _head_major(ref, page):
    return jnp.concatenate(
        [ref[pl.ds(h, page, stride=GROUP_HEADS), :] for h in range(GROUP_HEADS)], axis=1)


def _decode_kernel(pt_ref, thr_ref, qsb_ref, qmb_ref, vnew_ref, qmb8_ref, knew8_ref, rbt_ref,
                   ksb_ref, vsb_ref, kmb_ref, vmb_ref, osb_ref, omb_ref,
                   qsb_bd, qmb_bd, carry_ref, acc_sb, m_ref, l_ref, acc_mb, ksum_ref, bias_ref,
                   *, n_pages, page):
    bi = pl.program_id(0)
    s = pl.program_id(1)
    p = n_pages - 1 - s
    pages_per_block = MOBA_BLOCK // page
    n_blocks = n_pages // pages_per_block

    @pl.when((bi == 0) & (s == 0))
    def _():
        lane = lax.broadcasted_iota(I32, (GROUP_HEADS, page), 1)
        for t in range(pages_per_block):
            d = (pages_per_block - t) * page - lane
            bias_ref[t] = _bias_from_dist(d, thr_ref, lambda k: rbt_ref[:, k:k + 1])

    @pl.when(s == 0)
    def _():
        qsb_bd[...] = _block_diag(qsb_ref[...], BF16)
        qmb_bd[...] = _block_diag(qmb_ref[...], BF16)
        carry_ref[...] = jnp.zeros_like(carry_ref)
        acc_sb[...] = jnp.zeros_like(acc_sb)

    u = _suffix_matrix(page)

    z = _dot_nt(qsb_bd[...], _head_major(ksb_ref, page).astype(BF16)) * ATTN_SCALE
    sp = _softplus(z)
    ls = -sp
    loc = _suffix_sum(ls, u)
    carry = carry_ref[...]
    a = jnp.exp(z - sp + (loc + carry[:, :1]))
    acc_sb[...] += _dot(a.astype(BF16), _head_major(vsb_ref, page).astype(BF16))
    carry_ref[...] = carry + (loc[:, :1] + ls[:, :1])

    ksum_ref[p] = jnp.sum(kmb_ref[...].reshape(page, GROUP_HEADS, HEAD_DIM), axis=0)
    far = rbt_ref[:, REL_BUCKETS - 1:REL_BUCKETS]
    t_last = jnp.maximum(p - (n_pages - pages_per_block), 0)
    bias = jnp.where(p >= n_pages - pages_per_block, bias_ref[t_last], far)
    zm = _dot_nt(qmb_bd[...], _head_major(kmb_ref, page).astype(BF16)) * ATTN_SCALE + bias
    mp = jnp.max(zm, axis=1, keepdims=True)
    pm = jnp.exp(zm - mp)
    m_ref[p] = jnp.broadcast_to(mp, m_ref.shape[1:])
    l_ref[p] = jnp.broadcast_to(jnp.sum(pm, axis=1, keepdims=True), l_ref.shape[1:])
    acc_mb[p] = _dot(pm.astype(BF16), _head_major(vmb_ref, page).astype(BF16))

    @pl.when(s == n_pages - 1)
    def _():
        osb_ref[...] = _diag_rows(acc_sb[...])

        qf = qmb8_ref[...]
        gates = []
        for n in range(n_blocks):
            ks = ksum_ref[n * pages_per_block]
            for t in range(1, pages_per_block):
                ks = ks + ksum_ref[n * pages_per_block + t]
            gates.append(jnp.sum(qf * (ks * (1.0 / MOBA_BLOCK)), axis=1, keepdims=True))
        sels = []
        for n in range(n_blocks):
            rank = jnp.zeros(gates[n].shape, I32)
            for mm in range(n_blocks):
                if mm == n:
                    continue
                beats = (gates[mm] > gates[n]) | ((gates[mm] == gates[n]) & (mm < n))
                rank = rank + beats.astype(I32)
            sels.append(rank < MOBA_TOPK)

        z_own = jnp.sum(qf * knew8_ref[...], axis=1, keepdims=True) * ATTN_SCALE + rbt_ref[:, 0:1]
        m_all = z_own
        for pg in range(n_pages):
            m_all = jnp.maximum(m_all, jnp.where(sels[pg // pages_per_block], m_ref[pg][:, :1], -jnp.inf))
        w_own = jnp.exp(z_own - m_all)
        l_all = w_own
        o_all = w_own * jnp.broadcast_to(vnew_ref[...], (GROUP_HEADS, GROUP_WIDTH))
        for pg in range(n_pages):
            w = jnp.where(sels[pg // pages_per_block], jnp.exp(m_ref[pg][:, :1] - m_all), 0.0)
            l_all = l_all + w * l_ref[pg][:, :1]
            o_all = o_all + w * acc_mb[pg]
        omb_ref[...] = _diag_rows(o_all / l_all)


def _decode_attention(qkv_s, caches, page_table, thr, rel_bias, layer):
    db, n_pages = page_table.shape
    n_pool, n_layers, page = caches[0].shape[:3]
    w = GROUP_WIDTH
    q4 = qkv_s.reshape(6, db, 1, w)
    q8 = qkv_s.reshape(6, db, GROUP_HEADS, HEAD_DIM)
    caches = [c.reshape(n_pool, n_layers, page * GROUP_HEADS, HEAD_DIM) for c in caches]
    rbt = rel_bias.T
    pages_per_block = MOBA_BLOCK // page

    def row_spec(slab):
        return pl.BlockSpec((None, None, 1, w), lambda b, s, pt, thr: (slab, b, 0, 0))

    def head_spec(slab):
        return pl.BlockSpec((None, None, GROUP_HEADS, HEAD_DIM), lambda b, s, pt, thr: (slab, b, 0, 0))

    cache_spec = pl.BlockSpec(
        (None, None, page * GROUP_HEADS, HEAD_DIM),
        lambda b, s, pt, thr: (pt[b, n_pages - 1 - s], layer, 0, 0))
    out_spec = pl.BlockSpec((None, 1, w), lambda b, s, pt, thr: (b, 0, 0))
    grid_spec = pltpu.PrefetchScalarGridSpec(
        num_scalar_prefetch=2,
        grid=(db, n_pages),
        in_specs=[row_spec(0), row_spec(3), row_spec(5), head_spec(3), head_spec(4),
                  pl.BlockSpec((GROUP_HEADS, REL_BUCKETS), lambda b, s, pt, thr: (0, 0)),
                  cache_spec, cache_spec, cache_spec, cache_spec],
        out_specs=[out_spec, out_spec],
        scratch_shapes=[
            pltpu.VMEM((GROUP_HEADS, w), BF16),
            pltpu.VMEM((GROUP_HEADS, w), BF16),
            pltpu.VMEM((GROUP_HEADS, HEAD_DIM), F32),
            pltpu.VMEM((GROUP_HEADS, w), F32),
            pltpu.VMEM((n_pages, GROUP_HEADS, HEAD_DIM), F32),
            pltpu.VMEM((n_pages, GROUP_HEADS, HEAD_DIM), F32),
            pltpu.VMEM((n_pages, GROUP_HEADS, w), F32),
            pltpu.VMEM((n_pages, GROUP_HEADS, HEAD_DIM), F32),
            pltpu.VMEM((pages_per_block, GROUP_HEADS, page), F32),
        ],
    )
    vmem = 2 * 4 * page * w * 4 + 8 * page * w * 4 + n_pages * GROUP_HEADS * w * 4
    osb, omb = pl.pallas_call(
        functools.partial(_decode_kernel, n_pages=n_pages, page=page),
        grid_spec=grid_spec,
        out_shape=[jax.ShapeDtypeStruct((db, 1, w), F32)] * 2,
        compiler_params=_params(vmem + (12 << 20), 2),
        name="decode_attention",
    )(page_table, thr, q4, q4, q4, q8, q8, rbt, *caches)
    return osb, omb


def _post_attn_kernel(sbo_ref, mbo_ref, x_ref, g1_ref, sh2_ref, sc2_ref, sbg_ref, mbg_ref, nfg_ref,
                      wout_ref, rw_ref, rb_ref, x1_ref, h2_ref, te_ref, tg_ref):
    sb = (_rms(sbo_ref[...]) * sbg_ref[...]).astype(BF16)
    mb = (_rms(mbo_ref[...]) * mbg_ref[...]).astype(BF16)
    mix = _dot(sb, wout_ref[:GROUP_WIDTH, :]) + _dot(mb, wout_ref[GROUP_WIDTH:, :])
    x1 = x_ref[...] + g1_ref[...] * mix
    x1_ref[...] = x1
    h2 = (_rms(x1) * nfg_ref[...]) * (1.0 + sc2_ref[...]) + sh2_ref[...]
    h2_ref[...] = h2

    logits = _dot_split(h2, rw_ref[...]) + rb_ref[...]
    cols = lax.broadcasted_iota(I32, logits.shape, 1)
    c4 = lax.broadcasted_iota(I32, (logits.shape[0], TOP_K), 1)
    top_e = jnp.zeros((logits.shape[0], TOP_K), I32)
    top_v = jnp.zeros((logits.shape[0], TOP_K), F32)
    for k in range(TOP_K):
        mx = jnp.max(logits, axis=1, keepdims=True)
        idx = jnp.min(jnp.where(logits == mx, cols, N_EXPERTS), axis=1, keepdims=True)
        top_e = jnp.where(c4 == k, idx, top_e)
        top_v = jnp.where(c4 == k, mx, top_v)
        logits = jnp.where(cols == idx, -jnp.inf, logits)
    ex = jnp.exp(top_v - top_v[:, :1])
    te_ref[...] = top_e
    tg_ref[...] = ex / jnp.sum(ex, axis=1, keepdims=True)


def _post_attention(sbo, mbo, x3, g1, sh2, sc2, sb_out_g, moba_out_g, norm_ffn_g, w_out, router_w, router_b,
                    layer, tm):
    g, r, d = x3.shape
    w = GROUP_WIDTH
    rm = g1.shape[1]
    mod_block = (None, 1, d) if rm == 1 else (None, tm, d)
    mod_map = (lambda b, i: (b, 0, 0)) if rm == 1 else (lambda b, i: (b, i, 0))
    n_layers = w_out.shape[0]

    def row(width):
        return pl.BlockSpec((None, tm, width), lambda b, i: (b, i, 0))

    def per_layer(width):
        return pl.BlockSpec((None, 1, width), lambda b, i: (layer, 0, 0))

    vmem = 2 * (2 * w * d * 2) + 2 * tm * (2 * w + 3 * d) * 4 + 2 * tm * d * 4 * 2 + 8 * tm * d * 4
    return pl.pallas_call(
        _post_attn_kernel,
        grid=(g, r // tm),
        in_specs=[row(w), row(w), row(d),
                  pl.BlockSpec(mod_block, mod_map), pl.BlockSpec(mod_block, mod_map),
                  pl.BlockSpec(mod_block, mod_map),
                  per_layer(w), per_layer(w), per_layer(d),
                  pl.BlockSpec((None, 2 * w, d), lambda b, i: (layer, 0, 0)),
                  pl.BlockSpec((None, d, N_EXPERTS), lambda b, i: (layer, 0, 0)),
                  per_layer(N_EXPERTS)],
        out_specs=[row(d), row(d), row(TOP_K), row(TOP_K)],
        out_shape=[jax.ShapeDtypeStruct((g, r, d), F32), jax.ShapeDtypeStruct((g, r, d), F32),
                   jax.ShapeDtypeStruct((g, r, TOP_K), I32), jax.ShapeDtypeStruct((g, r, TOP_K), F32)],
        compiler_params=_params(vmem + (8 << 20), 2),
        name="post_attention",
    )(sbo, mbo, x3, g1, sh2, sc2,
      sb_out_g.reshape(n_layers, 1, w), moba_out_g.reshape(n_layers, 1, w),
      norm_ffn_g.reshape(n_layers, 1, d), w_out, router_w, router_b.reshape(n_layers, 1, N_EXPERTS))


def _gather_kernel(tok_ref, x_hbm, o_ref, buf, sem, *, rows):
    def issue(r, _):
        pltpu.make_async_copy(x_hbm.at[pl.ds(tok_ref[0, 0, r], 1)], buf.at[pl.ds(r, 1)], sem).start()
        return 0

    lax.fori_loop(0, rows, issue, 0)
    pltpu.make_async_copy(x_hbm.at[pl.ds(0, rows)], buf, sem).wait()
    o_ref[...] = buf[...].astype(o_ref.dtype)


def _gather_rows(x_pad, tok_of_row):
    d = x_pad.shape[1]
    n_rows = tok_of_row.shape[0]
    rows = MOE_ROWS
    n_blocks = n_rows // rows
    vmem = rows * d * 4 + 2 * rows * d * 2 + rows * d * 4
    return pl.pallas_call(
        functools.partial(_gather_kernel, rows=rows),
        grid=(n_blocks,),
        in_specs=[pl.BlockSpec((1, 1, rows), lambda i: (i, 0, 0), memory_space=pltpu.SMEM),
                  pl.BlockSpec(memory_space=pl.ANY)],
        out_specs=pl.BlockSpec((rows, d), lambda i: (i, 0)),
        out_shape=jax.ShapeDtypeStruct((n_rows, d), BF16),
        scratch_shapes=[pltpu.VMEM((rows, d), F32), pltpu.SemaphoreType.DMA(())],
        compiler_params=_params(vmem + (4 << 20), 1),
        name="moe_gather",
    )(tok_of_row.reshape(n_blocks, 1, rows), x_pad)


def _expert_changed(be_ref, i):
    return (i == 0) | (be_ref[i] != be_ref[jnp.maximum(i - 1, 0)])


def _gate_up_kernel(be_ref, nu_ref, x_ref, wg_ref, wu_ref, bg_ref, bu_ref, o_ref, wg_bf, wu_bf):
    i = pl.program_id(1)

    @pl.when(_expert_changed(be_ref, i))
    def _():
        wg_bf[...] = wg_ref[...].astype(BF16)
        wu_bf[...] = wu_ref[...].astype(BF16)

    @pl.when(i < nu_ref[0])
    def _():
        x = x_ref[...]
        g = jnp.minimum(_dot(x, wg_bf[...]) + bg_ref[...], SWIGLU_LIMIT)
        u = jnp.clip(_dot(x, wu_bf[...]) + bu_ref[...], -SWIGLU_LIMIT, SWIGLU_LIMIT)
        o_ref[...] = (g * jax.nn.sigmoid(SWIGLU_ALPHA * g) * (u + 1.0)).astype(o_ref.dtype)

    @pl.when(i >= nu_ref[0])
    def _():
        o_ref[...] = jnp.zeros_like(o_ref)


def _gate_up(xs, block_e, n_used, w_gu, b_gu, layer):
    n_rows, d = xs.shape
    d_ff = w_gu.shape[3] // 2
    rows = MOE_ROWS
    n_blocks = n_rows // rows
    tf = 512
    n_tiles = d_ff // tf
    n_layers = w_gu.shape[0]
    vmem = 2 * (2 * d * tf * 4 + rows * d * 2 + rows * tf * 2) + 2 * d * tf * 2 + 6 * rows * tf * 4
    grid_spec = pltpu.PrefetchScalarGridSpec(
        num_scalar_prefetch=2,
        grid=(n_tiles, n_blocks),
        in_specs=[
            pl.BlockSpec((rows, d), lambda j, i, be, nu: (i, 0)),
            pl.BlockSpec((None, None, d, tf), lambda j, i, be, nu: (layer, be[i], 0, j)),
            pl.BlockSpec((None, None, d, tf), lambda j, i, be, nu: (layer, be[i], 0, n_tiles + j)),
            pl.BlockSpec((None, None, 1, tf), lambda j, i, be, nu: (layer, be[i], 0, j)),
            pl.BlockSpec((None, None, 1, tf), lambda j, i, be, nu: (layer, be[i], 0, n_tiles + j)),
        ],
        out_specs=pl.BlockSpec((rows, tf), lambda j, i, be, nu: (i, j)),
        scratch_shapes=[pltpu.VMEM((d, tf), BF16), pltpu.VMEM((d, tf), BF16)],
    )
    b4 = b_gu.reshape(n_layers, N_EXPERTS, 1, 2 * d_ff)
    return pl.pallas_call(
        _gate_up_kernel,
        grid_spec=grid_spec,
        out_shape=jax.ShapeDtypeStruct((n_rows, d_ff), BF16),
        compiler_params=_params(vmem + (8 << 20), 2),
        name="moe_gate_up",
    )(block_e, n_used, xs, w_gu, w_gu, b4, b4)


def _down_kernel(be_ref, nu_ref, a_ref, w_ref, b_ref, o_ref, w_bf):
    i = pl.program_id(1)

    @pl.when(_expert_changed(be_ref, i))
    def _():
        w_bf[...] = w_ref[...].astype(BF16)

    @pl.when(i < nu_ref[0])
    def _():
        o_ref[...] = _dot(a_ref[...], w_bf[...]) + b_ref[...]

    @pl.when(i >= nu_ref[0])
    def _():
        o_ref[...] = jnp.zeros_like(o_ref)


def _down(act, block_e, n_used, w_down, b_down, layer):
    n_rows, d_ff = act.shape
    d = w_down.shape[3]
    rows = MOE_ROWS
    n_blocks = n_rows // rows
    tn = 1024
    n_layers = w_down.shape[0]
    vmem = 2 * (d_ff * tn * 4 + rows * d_ff * 2 + rows * tn * 4) + d_ff * tn * 2 + 2 * rows * tn * 4
    grid_spec = pltpu.PrefetchScalarGridSpec(
        num_scalar_prefetch=2,
        grid=(d // tn, n_blocks),
        in_specs=[
            pl.BlockSpec((rows, d_ff), lambda j, i, be, nu: (i, 0)),
            pl.BlockSpec((None, None, d_ff, tn), lambda j, i, be, nu: (layer, be[i], 0, j)),
            pl.BlockSpec((None, None, 1, tn), lambda j, i, be, nu: (layer, be[i], 0, j)),
        ],
        out_specs=pl.BlockSpec((rows, tn), lambda j, i, be, nu: (i, j)),
        scratch_shapes=[pltpu.VMEM((d_ff, tn), BF16)],
    )
    return pl.pallas_call(
        _down_kernel,
        grid_spec=grid_spec,
        out_shape=jax.ShapeDtypeStruct((n_rows, d), F32),
        compiler_params=_params(vmem + (8 << 20), 2),
        name="moe_down",
    )(block_e, n_used, act, w_down, b_down.reshape(n_layers, N_EXPERTS, 1, d))


def _combine_kernel(dest_ref, y_hbm, gate_ref, x1_ref, g2_ref, fg_ref, o_ref, buf, sem, *, tokens, final):
    def issue(t, _):
        for k in range(TOP_K):
            pltpu.make_async_copy(
                y_hbm.at[pl.ds(dest_ref[0, 0, t * TOP_K + k], 1)], buf.at[k, pl.ds(t, 1)], sem).start()
        return 0

    lax.fori_loop(0, tokens, issue, 0)
    for k in range(TOP_K):
        pltpu.make_async_copy(y_hbm.at[pl.ds(0, tokens)], buf.at[k], sem).wait()
    gates = gate_ref[...]
    y = gates[:, 0:1] * buf[0]
    for k in range(1, TOP_K):
        y = y + gates[:, k:k + 1] * buf[k]
    x2 = x1_ref[...] + g2_ref[...] * y
    if final:
        x2 = _rms(x2) * fg_ref[...]
    o_ref[...] = x2


def _combine(yb, dest, gates, x1, g2, final_g, final, tokens):
    g, r, d = x1.shape
    rm = g2.shape[1]
    mod_block = (None, 1, d) if rm == 1 else (None, tokens, d)
    mod_map = (lambda b, i: (b, 0, 0)) if rm == 1 else (lambda b, i: (b, i, 0))
    n_tiles = r // tokens
    vmem = TOP_K * tokens * d * 4 + 2 * 3 * tokens * d * 4 + 4 * tokens * d * 4
    return pl.pallas_call(
        functools.partial(_combine_kernel, tokens=tokens, final=final),
        grid=(g, n_tiles),
        in_specs=[
            pl.BlockSpec((None, 1, 1, tokens * TOP_K), lambda b, i: (b, i, 0, 0), memory_space=pltpu.SMEM),
            pl.BlockSpec(memory_space=pl.ANY),
            pl.BlockSpec((None, tokens, TOP_K), lambda b, i: (b, i, 0)),
            pl.BlockSpec((None, tokens, d), lambda b, i: (b, i, 0)),
            pl.BlockSpec(mod_block, mod_map),
            pl.BlockSpec((1, d), lambda b, i: (0, 0)),
        ],
        out_specs=pl.BlockSpec((None, tokens, d), lambda b, i: (b, i, 0)),
        out_shape=jax.ShapeDtypeStruct((g, r, d), F32),
        scratch_shapes=[pltpu.VMEM((TOP_K, tokens, d), F32), pltpu.SemaphoreType.DMA(())],
        compiler_params=_params(vmem + (4 << 20), 2),
        name="moe_combine",
    )(dest.reshape(g, n_tiles, 1, tokens * TOP_K), yb, gates, x1, g2, final_g.reshape(1, d))


def _route(top_e, n_tokens):
    flat_e = top_e.reshape(-1)
    n_pairs = flat_e.shape[0]
    onehot = (flat_e[:, None] == jnp.arange(N_EXPERTS, dtype=I32)[None, :]).astype(I32)
    csum = jnp.cumsum(onehot, axis=0)
    rank = jnp.sum(csum * onehot, axis=1) - 1
    counts = csum[-1]
    padded = (counts + MOE_ROWS - 1) // MOE_ROWS * MOE_ROWS
    pend = jnp.cumsum(padded)
    pstart = pend - padded
    dest = pstart[flat_e] + rank
    n_blocks = -(-(n_pairs + N_EXPERTS * (MOE_ROWS - 1)) // MOE_ROWS)
    n_rows = n_blocks * MOE_ROWS
    flat_t = jnp.arange(n_pairs, dtype=I32) // TOP_K
    tok_of_row = jnp.full((n_rows,), n_tokens, I32).at[dest].set(flat_t)
    block_e = jnp.minimum(
        jnp.searchsorted(pend, jnp.arange(n_blocks, dtype=I32) * MOE_ROWS, side="right"),
        N_EXPERTS - 1).astype(I32)
    n_used = (pend[-1] // MOE_ROWS).astype(I32).reshape(1)
    return dest.astype(I32), tok_of_row, block_e, n_used


def _bucket_thresholds():
    max_exact = REL_BUCKETS // 2
    n = np.arange(2 * MOBA_BLOCK)
    nf = np.maximum(n, 1).astype(np.float64)
    large = max_exact + (np.log(nf / max_exact) / math.log(REL_MAX_DIST / max_exact)
                         * (REL_BUCKETS - max_exact)).astype(np.int64)
    bucket = np.where(n < max_exact, n, np.minimum(large, REL_BUCKETS - 1))
    assert np.all(np.diff(bucket) >= 0) and np.all(bucket[MOBA_BLOCK:] == REL_BUCKETS - 1)
    thr = np.sum(bucket[None, :] < np.arange(REL_BUCKETS)[:, None], axis=1)
    return jnp.asarray(thr, dtype=I32)


def kernel(x_prompt, x_sample, cache_sb_k, cache_sb_v, cache_moba_k, cache_moba_v, page_table, c_prompt,
           c_sample, w_ada, b_ada, norm_attn_g, norm_ffn_g, w_in, sb_out_g, moba_out_g, w_out, rel_bias,
           router_w, router_b, w_gu, b_gu, w_down, b_down, final_g):
    n_layers = w_in.shape[0]
    b, t, d = x_prompt.shape
    db = x_sample.shape[0]
    assert x_sample.shape[1] == 1 and t % MOBA_BLOCK == 0 and t % SB_TILE == 0
    assert (page_table.shape[1] * cache_sb_k.shape[2]) % MOBA_BLOCK == 0
    n_prompt = b * t
    n_tokens = n_prompt + db

    thr = _bucket_thresholds()
    w_out = w_out.astype(BF16)
    pad_rows = -(b + db) % 8
    c_all = jnp.concatenate([c_prompt, c_sample, jnp.zeros((pad_rows, d), F32)], axis=0)
    ada = _ada_params(c_all, w_ada, b_ada)

    xp = x_prompt
    xs = x_sample.reshape(1, db, d)
    caches = [cache_sb_k, cache_sb_v, cache_moba_k, cache_moba_v]
    kv_p, kv_s = [], []
    for layer in range(n_layers):
        mod_p = [ada[layer, :b, i * d:(i + 1) * d].reshape(b, 1, d) for i in range(6)]
        mod_s = [ada[layer, b:b + db, i * d:(i + 1) * d].reshape(1, db, d) for i in range(6)]
        last = layer == n_layers - 1

        hp = _modulate(xp, mod_p[0], mod_p[1], norm_attn_g, layer, 512)
        hs = _modulate(xs, mod_s[0], mod_s[1], norm_attn_g, layer, db)
        qkv_p = _qkv_proj(hp.reshape(n_prompt, d), w_in, layer, 512)
        qkv_s = _qkv_proj(hs.reshape(db, d), w_in, layer, db)
        kv_p.append(qkv_p)
        kv_s.append(qkv_s)

        qkv4 = qkv_p.reshape(6, b, t, GROUP_WIDTH)
        sbo_p = _sb_attention(qkv4, SB_TILE)
        mbo_p = _moba_attention(qkv4, thr, rel_bias)
        sbo_s, mbo_s = _decode_attention(qkv_s, caches, page_table, thr, rel_bias, layer)

        x1p, h2p, tep, tgp = _post_attention(
            sbo_p, mbo_p, xp, mod_p[2], mod_p[3], mod_p[4], sb_out_g, moba_out_g, norm_ffn_g, w_out,
            router_w, router_b, layer, 256)
        x1s, h2s, tes, tgs = _post_attention(
            sbo_s.reshape(1, db, GROUP_WIDTH), mbo_s.reshape(1, db, GROUP_WIDTH), xs, mod_s[2], mod_s[3],
            mod_s[4], sb_out_g, moba_out_g, norm_ffn_g, w_out, router_w, router_b, layer, db)

        top_e = jnp.concatenate([tep.reshape(n_prompt, TOP_K), tes.reshape(db, TOP_K)], axis=0)
        dest, tok_of_row, block_e, n_used = _route(top_e, n_tokens)
        h2 = jnp.concatenate([h2p.reshape(n_prompt, d), h2s.reshape(db, d), jnp.zeros((8, d), F32)], axis=0)
        rows = _gather_rows(h2, tok_of_row)
        act = _gate_up(rows, block_e, n_used, w_gu, b_gu, layer)
        yb = _down(act, block_e, n_used, w_down, b_down, layer)

        dest2 = dest.reshape(n_tokens, TOP_K)
        xp = _combine(yb, dest2[:n_prompt].reshape(b, t, TOP_K), tgp, x1p, mod_p[5], final_g, last, 64)
        xs = _combine(yb, dest2[n_prompt:].reshape(1, db, TOP_K), tgs, x1s, mod_s[5], final_g, last, 64)

    def stack(kvs, slab, lead):
        return jnp.stack([kv[slab].reshape(lead, -1, GROUP_HEADS, HEAD_DIM) for kv in kvs], axis=1)

    return (xp, xs.reshape(db, 1, d),
            stack(kv_p, 1, b), stack(kv_p, 2, b), stack(kv_p, 4, b), stack(kv_p, 5, b),
            stack(kv_s, 1, db), stack(kv_s, 2, db), stack(kv_s, 4, db), stack(kv_s, 5, db))
```
